```python
import math
import jax, jax.numpy as jnp
from jax import lax
import numpy as np

D_MODEL = 1024
BATCH = 8
SEQ = 2048
DEPTH = 2

SSD_EXPAND = 2
SSD_D_INNER = SSD_EXPAND * D_MODEL
SSD_HEAD_DIM = 64
SSD_N_HEADS = SSD_D_INNER // SSD_HEAD_DIM
SSD_N_GROUPS = 4
SSD_HEADS_PER_GROUP = SSD_N_HEADS // SSD_N_GROUPS
SSD_D_STATE = 128
SSD_CONV_WIDTH = 4
SSD_CHUNK = 128
SSD_CONV_DIM = SSD_D_INNER + 2 * SSD_N_GROUPS * SSD_D_STATE

S5_WIDTH = D_MODEL
S5_GROUP = 16
S5_N_GROUPS = S5_WIDTH // S5_GROUP
S5_STATE = 64
S5_MIN_STEP = 0.001
S5_MAX_STEP = 0.1
S5_MAX_REAL = -1e-4

FFN_HIDDEN = 2816

RMS_EPS = 1e-6

IN_PROJ_DIM = SSD_D_INNER + SSD_CONV_DIM + SSD_N_HEADS + S5_WIDTH + 2 * D_MODEL
SPLITS = list(np.cumsum([SSD_D_INNER, SSD_CONV_DIM, SSD_N_HEADS, S5_WIDTH, D_MODEL]))

kernel_name = "hybrid_macaron_gated_ssd_s5"


def rmsnorm(x, g):
    xf = x.astype(jnp.float32)
    xf = xf * lax.rsqrt(jnp.mean(xf * xf, axis=-1, keepdims=True) + RMS_EPS)
    return (xf * g.astype(jnp.float32)).astype(x.dtype)


def swiglu(x, w_gate, w_up, w_down):
    return (jax.nn.silu(x @ w_gate) * (x @ w_up)) @ w_down


def causal_depthwise_conv(u, w, b):
    k, c = w.shape
    out = lax.conv_general_dilated(
        u, w[:, None, :].astype(u.dtype), window_strides=(1,), padding=[(k - 1, 0)],
        dimension_numbers=("NWC", "WIO", "NWC"), feature_group_count=c)
    return out + b.astype(u.dtype)


def segsum(a):
    t = a.shape[-1]
    cs = jnp.cumsum(a, axis=-1)
    seg = cs[..., :, None] - cs[..., None, :]
    mask = jnp.tril(jnp.ones((t, t), dtype=bool))
    return jnp.where(mask, seg, -jnp.inf)


def ssd_chunked(xdt, adt, bmat, cmat):
    b, l, h, p = xdt.shape
    g, n = bmat.shape[2], bmat.shape[3]
    r = h // g
    q = SSD_CHUNK
    c = l // q
    X = xdt.reshape(b, c, q, g, r, p)
    A = adt.reshape(b, c, q, g, r).transpose(0, 3, 4, 1, 2)
    Bc = bmat.reshape(b, c, q, g, n)
    Cc = cmat.reshape(b, c, q, g, n)
    a_cum = jnp.cumsum(A, axis=-1)
    decay_in = jnp.exp(segsum(A))
    cb = jnp.einsum("bclgn,bcsgn->bcgls", Cc, Bc)
    y_diag = jnp.einsum("bcgls,bgrcls,bcsgrp->bclgrp", cb, decay_in, X)
    decay_states = jnp.exp(a_cum[..., -1:] - a_cum)
    states = jnp.einsum("bclgn,bgrcl,bclgrp->bcgrpn", Bc, decay_states, X)
    states = jnp.concatenate([jnp.zeros_like(states[:, :1]), states], axis=1)
    chunk_a = jnp.pad(a_cum[..., -1], ((0, 0), (0, 0), (0, 0), (1, 0)))
    chunk_decay = jnp.exp(segsum(chunk_a))
    new_states = jnp.einsum("bgrzc,bcgrpn->bzgrpn", chunk_decay, states)
    states = new_states[:, :-1]
    y_off = jnp.einsum("bclgn,bcgrpn,bgrcl->bclgrp", Cc, states, jnp.exp(a_cum))
    return (y_diag + y_off).reshape(b, l, h, p)


def ssd_branch(z, xbc, dt_raw, conv_w, conv_b, dt_bias, a_log, d_skip, norm_g):
    b, l, _ = z.shape
    xbc = jax.nn.silu(causal_depthwise_conv(xbc, conv_w, conv_b)).astype(jnp.float32)
    xs = xbc[..., :SSD_D_INNER].reshape(b, l, SSD_N_HEADS, SSD_HEAD_DIM)
    bm = xbc[..., SSD_D_INNER:SSD_D_INNER + SSD_N_GROUPS * SSD_D_STATE].reshape(b, l, SSD_N_GROUPS, SSD_D_STATE)
    cm = xbc[..., SSD_D_INNER + SSD_N_GROUPS * SSD_D_STATE:].reshape(b, l, SSD_N_GROUPS, SSD_D_STATE)
    dt = jax.nn.softplus(dt_raw.astype(jnp.float32) + dt_bias.astype(jnp.float32))
    a = -jnp.exp(a_log.astype(jnp.float32))
    y = ssd_chunked(xs * dt[..., None], a * dt, bm, cm) + d_skip.astype(jnp.float32)[:, None] * xs
    y = y.reshape(b, l, SSD_D_INNER)
    y = y * jax.nn.silu(z.astype(jnp.float32))
    yg = y.reshape(b, l, SSD_N_GROUPS, SSD_D_INNER // SSD_N_GROUPS)
    yg = yg * lax.rsqrt(jnp.mean(yg * yg, axis=-1, keepdims=True) + RMS_EPS)
    y = yg.reshape(b, l, SSD_D_INNER) * norm_g.astype(jnp.float32)
    return y.astype(z.dtype)


def s5_branch(u, lam_re, lam_im, b_re, b_im, c_re, c_im, log_step, d_skip, w_glu):
    b, l, _ = u.shape
    uf = u.astype(jnp.float32)
    ug = uf.reshape(b, l, S5_N_GROUPS, S5_GROUP)
    lam = lax.complex(jnp.minimum(lam_re.astype(jnp.float32), S5_MAX_REAL), lam_im.astype(jnp.float32))
    step = jnp.exp(log_step.astype(jnp.float32))[:, None]
    lam_bar = jnp.exp(lam * step)
    b_bar = ((lam_bar - 1.0) / lam)[..., None] * lax.complex(b_re.astype(jnp.float32), b_im.astype(jnp.float32))
    bu = jnp.einsum("blgc,gnc->blgn", ug.astype(jnp.complex64), b_bar)
    a_elems = jnp.broadcast_to(lam_bar, bu.shape)

    def combine(e1, e2):
        a1, s1 = e1
        a2, s2 = e2
        return a2 * a1, a2 * s1 + s2

    _, states = lax.associative_scan(combine, (a_elems, bu), axis=1)
    c = lax.complex(c_re.astype(jnp.float32), c_im.astype(jnp.float32))
    y = jnp.real(jnp.einsum("blgn,gcn->blgc", states, c)).reshape(b, l, S5_WIDTH)
    y = y + d_skip.astype(jnp.float32) * uf
    y = jax.nn.gelu(y).astype(u.dtype)
    val, gate = jnp.split(y @ w_glu, 2, axis=-1)
    return val * jax.nn.sigmoid(gate)


def mixer(u, w_in, ssd_conv_w, ssd_conv_b, ssd_dt_bias, ssd_a_log, ssd_d, ssd_norm_g, w_branch_a,
          s5_lambda_re, s5_lambda_im, s5_b_re, s5_b_im, s5_c_re, s5_c_im, s5_log_step, s5_d, s5_w_glu,
          w_branch_b, w_out):
    proj = u @ w_in
    z, xbc, dt_raw, u5, g_a, g_b = jnp.split(proj, SPLITS, axis=-1)
    y_a = ssd_branch(z, xbc, dt_raw, ssd_conv_w, ssd_conv_b, ssd_dt_bias, ssd_a_log, ssd_d, ssd_norm_g) @ w_branch_a
    y_b = s5_branch(u5, s5_lambda_re, s5_lambda_im, s5_b_re, s5_b_im, s5_c_re, s5_c_im,
                    s5_log_step, s5_d, s5_w_glu) @ w_branch_b
    merged = jax.nn.sigmoid(g_a) * y_a + jax.nn.sigmoid(g_b) * y_b
    return merged @ w_out


def setup_inputs(seed: int = 0) -> dict:
    key = jax.random.key(seed)
    ks = iter(jax.random.split(key, 40))
    f32 = jnp.float32
    L = DEPTH

    def normal(shape, scale):
        return jax.random.normal(next(ks), shape, f32) * scale

    def gain(shape):
        return 1.0 + 0.02 * jax.random.normal(next(ks), shape, f32)

    x = jax.random.normal(next(ks), (BATCH, SEQ, D_MODEL), f32)
    d_ = D_MODEL
    dt_init = jnp.exp(jax.random.uniform(next(ks), (L, SSD_N_HEADS), f32,
                                         math.log(0.001), math.log(0.1)))
    n_idx = jnp.arange(S5_STATE, dtype=f32)
    return {
        "x": x,
        "ffn1_pre_g": gain((L, d_)),
        "ffn1_post_g": gain((L, d_)),
        "ffn1_w_gate": normal((L, d_, FFN_HIDDEN), d_ ** -0.5),
        "ffn1_w_up": normal((L, d_, FFN_HIDDEN), d_ ** -0.5),
        "ffn1_w_down": normal((L, FFN_HIDDEN, d_), FFN_HIDDEN ** -0.5),
        "mix_pre_g": gain((L, d_)),
        "mix_post_g": gain((L, d_)),
        "w_in": normal((L, d_, IN_PROJ_DIM), d_ ** -0.5),
        "ssd_conv_w": normal((L, SSD_CONV_WIDTH, SSD_CONV_DIM), SSD_CONV_WIDTH ** -0.5),
        "ssd_conv_b": normal((L, SSD_CONV_DIM), 0.02),
        "ssd_dt_bias": dt_init + jnp.log(-jnp.expm1(-dt_init)),
        "ssd_a_log": jnp.log(jax.random.uniform(next(ks), (L, SSD_N_HEADS), f32, 1.0, 16.0)),
        "ssd_d": gain((L, SSD_N_HEADS)),
        "ssd_norm_g": gain((L, SSD_D_INNER)),
        "w_branch_a": normal((L, SSD_D_INNER, d_), SSD_D_INNER ** -0.5),
        "s5_lambda_re": -0.5 + 0.01 * jax.random.normal(next(ks), (L, S5_N_GROUPS, S5_STATE), f32),
        "s5_lambda_im": math.pi * n_idx + 0.01 * jax.random.normal(next(ks), (L, S5_N_GROUPS, S5_STATE), f32),
        "s5_b_re": normal((L, S5_N_GROUPS, S5_STATE, S5_GROUP), (2 * S5_GROUP) ** -0.5),
        "s5_b_im": normal((L, S5_N_GROUPS, S5_STATE, S5_GROUP), (2 * S5_GROUP) ** -0.5),
        "s5_c_re": normal((L, S5_N_GROUPS, S5_GROUP, S5_STATE), (2 * S5_STATE) ** -0.5),
        "s5_c_im": normal((L, S5_N_GROUPS, S5_GROUP, S5_STATE), (2 * S5_STATE) ** -0.5),
        "s5_log_step": jax.random.uniform(next(ks), (L, S5_N_GROUPS), f32,
                                          math.log(S5_MIN_STEP), math.log(S5_MAX_STEP)),
        "s5_d": normal((L, S5_WIDTH), 1.0),
        "s5_w_glu": normal((L, S5_WIDTH, 2 * S5_WIDTH), S5_WIDTH ** -0.5),
        "w_branch_b": normal((L, S5_WIDTH, d_), S5_WIDTH ** -0.5),
        "w_out": normal((L, d_, d_), d_ ** -0.5),
        "ffn2_pre_g": gain((L, d_)),
        "ffn2_post_g": gain((L, d_)),
        "ffn2_w_gate": normal((L, d_, FFN_HIDDEN), d_ ** -0.5),
        "ffn2_w_up": normal((L, d_, FFN_HIDDEN), d_ ** -0.5),
        "ffn2_w_down": normal((L, FFN_HIDDEN, d_), FFN_HIDDEN ** -0.5),
    }


def reference(x, ffn1_pre_g, ffn1_post_g, ffn1_w_gate, ffn1_w_up, ffn1_w_down,
              mix_pre_g, mix_post_g, w_in, ssd_conv_w, ssd_conv_b, ssd_dt_bias, ssd_a_log, ssd_d,
              ssd_norm_g, w_branch_a, s5_lambda_re, s5_lambda_im, s5_b_re, s5_b_im, s5_c_re, s5_c_im,
              s5_log_step, s5_d, s5_w_glu, w_branch_b, w_out,
              ffn2_pre_g, ffn2_post_g, ffn2_w_gate, ffn2_w_up, ffn2_w_down):
    h = x
    for i in range(DEPTH):
        f = swiglu(rmsnorm(h, ffn1_pre_g[i]), ffn1_w_gate[i], ffn1_w_up[i], ffn1_w_down[i])
        h = h + 0.5 * rmsnorm(f, ffn1_post_g[i])
        m = mixer(rmsnorm(h, mix_pre_g[i]), w_in[i], ssd_conv_w[i], ssd_conv_b[i], ssd_dt_bias[i],
                  ssd_a_log[i], ssd_d[i], ssd_norm_g[i], w_branch_a[i],
                  s5_lambda_re[i], s5_lambda_im[i], s5_b_re[i], s5_b_im[i], s5_c_re[i], s5_c_im[i],
                  s5_log_step[i], s5_d[i], s5_w_glu[i], w_branch_b[i], w_out[i])
        h = h + rmsnorm(m, mix_post_g[i])
        f = swiglu(rmsnorm(h, ffn2_pre_g[i]), ffn2_w_gate[i], ffn2_w_up[i], ffn2_w_down[i])
        h = h + 0.5 * rmsnorm(f, ffn2_post_g[i])
    return h
```

```python
import functools
import math

import jax
import jax.numpy as jnp
from jax import lax
from jax.experimental import pallas as pl
from jax.experimental.pallas import tpu as pltpu

F32 = jnp.float32
BF16 = jnp.bfloat16

RMS_EPS = 1e-6
SSD_HEAD_DIM = 64
SSD_N_GROUPS = 4
SSD_D_STATE = 128
SSD_CONV_WIDTH = 4
SSD_CHUNK = 128
S5_GROUP = 16
S5_STATE = 64
S5_MAX_REAL = -1e-4

LANES = 128
SUBLANES = 8
S5_SLAB_GROUPS = LANES // S5_GROUP
S5_SLAB_STATE = S5_SLAB_GROUPS * S5_STATE
S5_PERM_T = 32
VMEM_LIMIT = 56 * 1024 * 1024


def _rms(x, g):
    return x * lax.rsqrt(jnp.mean(x * x, axis=-1, keepdims=True) + RMS_EPS) * g


def _dot(a, b):
    return jnp.dot(a, b, preferred_element_type=F32)


def _split3(x):
    hi = x.astype(BF16)
    r1 = x - hi.astype(F32)
    mid = r1.astype(BF16)
    lo = (r1 - mid.astype(F32)).astype(BF16)
    return hi, mid, lo


def _dot01_left(m01, x):
    hi, mid, lo = _split3(x)
    return _dot(m01, hi) + _dot(m01, mid) + _dot(m01, lo)


def _dot01_right(x, m01):
    hi, mid, lo = _split3(x)
    return _dot(hi, m01) + _dot(mid, m01) + _dot(lo, m01)


def _softplus(x):
    return jnp.maximum(x, 0.0) + jnp.log1p(jnp.exp(-jnp.abs(x)))


def _resident(shape):
    nd = len(shape)
    return pl.BlockSpec(shape, lambda *_: (0,) * nd, pipeline_mode=pl.Buffered(1))


def _params(sem):
    return pltpu.CompilerParams(dimension_semantics=sem, vmem_limit_bytes=VMEM_LIMIT)


def _ffn_kernel(x_ref, pre_ref, post_ref, wg_ref, wu_ref, wd_ref, o_ref):
    x = x_ref[...]
    xn = _rms(x, pre_ref[...]).astype(BF16)
    g = _dot(xn, wg_ref[...])
    u = _dot(xn, wu_ref[...])
    a = (g * jax.nn.sigmoid(g) * u).astype(BF16)
    f = _dot(a, wd_ref[...])
    o_ref[...] = x + 0.5 * _rms(f, post_ref[...])


def _ffn(h, pre_g, post_g, wg, wu, wd, tm):
    m, d = h.shape
    hid = wg.shape[1]
    row = pl.BlockSpec((tm, d), lambda i: (i, 0))
    return pl.pallas_call(
        _ffn_kernel,
        grid=(m // tm,),
        in_specs=[row, _resident((1, d)), _resident((1, d)),
                  _resident((d, hid)), _resident((d, hid)), _resident((hid, d))],
        out_specs=row,
        out_shape=jax.ShapeDtypeStruct((m, d), F32),
        compiler_params=_params(("parallel",)),
        name="ffn",
    )(h, pre_g, post_g, wg, wu, wd)


def _inproj_kernel(h_ref, pre_ref, wz_ref, wx_ref, wdt_ref, wu_ref, wga_ref, wgb_ref,
                   z_ref, x_ref, dt_ref, u_ref, ga_ref, gb_ref):
    xn = _rms(h_ref[...], pre_ref[...]).astype(BF16)
    z_ref[...] = _dot(xn, wz_ref[...]).astype(BF16)
    x_ref[...] = _dot(xn, wx_ref[...]).astype(BF16)
    dt_ref[...] = _dot(xn, wdt_ref[...])
    u_ref[...] = _dot(xn, wu_ref[...]).astype(BF16)
    ga_ref[...] = _dot(xn, wga_ref[...]).astype(BF16)
    gb_ref[...] = _dot(xn, wgb_ref[...]).astype(BF16)


def _inproj(h, pre_g, ws, tm):
    m, d = h.shape
    widths = [w.shape[1] for w in ws]
    dtypes = [BF16, BF16, F32, BF16, BF16, BF16]
    row = pl.BlockSpec((tm, d), lambda i: (i, 0))
    return pl.pallas_call(
        _inproj_kernel,
        grid=(m // tm,),
        in_specs=[row, _resident((1, d))] + [_resident((d, n)) for n in widths],
        out_specs=[pl.BlockSpec((tm, n), lambda i: (i, 0)) for n in widths],
        out_shape=[jax.ShapeDtypeStruct((m, n), t) for n, t in zip(widths, dtypes)],
        compiler_params=_params(("parallel",)),
        name="in_proj",
    )(h, pre_g, *ws)


def _ssd_kernel(z_ref, xbc_ref, dt_ref, dtt_ref, cw_ref, cb_ref, bias_ref, biast_ref,
                alog_ref, alogt_ref, dexp_ref, ng_ref, tril_ref, triu_ref, e_ref,
                o_ref, xwin, state, *, d_inner, n_heads):
    q = SSD_CHUNK
    gw = d_inner // SSD_N_GROUPS
    hpg = n_heads // SSD_N_GROUPS
    halo = SUBLANES
    c = pl.program_id(1)

    @pl.when(c == 0)
    def _():
        xwin[0:halo, :] = jnp.zeros((halo, xwin.shape[1]), F32)
        state[...] = jnp.zeros(state.shape, F32)

    @pl.when(c > 0)
    def _():
        xwin[0:halo, :] = xwin[q:q + halo, :]

    xwin[halo:halo + q, :] = xbc_ref[0].astype(F32)

    def conv_silu(lo, width):
        acc = jnp.broadcast_to(cb_ref[:, lo:lo + width], (q, width))
        for k in range(SSD_CONV_WIDTH):
            off = halo - (SSD_CONV_WIDTH - 1) + k
            acc = acc + cw_ref[k:k + 1, lo:lo + width] * xwin[off:off + q, lo:lo + width]
        return acc * jax.nn.sigmoid(acc)

    a_neg = -jnp.exp(alog_ref[...])
    dt = _softplus(dt_ref[0] + bias_ref[...])
    a_cum = _dot01_left(tril_ref[...], dt * a_neg)
    dt_t = _softplus(dtt_ref[0] + biast_ref[...])
    a_cum_t = _dot01_right(dt_t * (-jnp.exp(alogt_ref[...])), triu_ref[...])
    a_last = a_cum[q - 1:q, :]
    e_acum = jnp.exp(a_cum)
    dt_decay = dt * jnp.exp(a_last - a_cum)

    row = lax.broadcasted_iota(jnp.int32, (q, q), 0)
    col = lax.broadcasted_iota(jnp.int32, (q, q), 1)
    causal = row >= col
    lane = lax.broadcasted_iota(jnp.int32, (q, 2 * SSD_HEAD_DIM), 1)
    first_head = lane < SSD_HEAD_DIM

    for g in range(SSD_N_GROUPS):
        e_g = e_ref[:, g * gw:(g + 1) * gw]
        dt_e = _dot01_right(dt, e_g)
        ea_e = _dot01_right(e_acum, e_g)
        dd_e = _dot01_right(dt_decay, e_g)

        xs = conv_silu(g * gw, gw)
        bm = conv_silu(d_inner + g * SSD_D_STATE, SSD_D_STATE).astype(BF16)
        cm = conv_silu(d_inner + (SSD_N_GROUPS + g) * SSD_D_STATE, SSD_D_STATE).astype(BF16)

        cb = lax.dot_general(cm, bm, (((1,), (1,)), ((), ())), preferred_element_type=F32)
        xdt = (xs * dt_e).astype(BF16)

        s_prev = state[g]
        y = _dot(cm, s_prev.astype(BF16)) * ea_e
        xdec = (xs * dd_e).astype(BF16)
        s_new = lax.dot_general(bm, xdec, (((0,), (0,)), ((), ())), preferred_element_type=F32)
        state[g] = s_prev * ea_e[q - 1:q, :] + s_new

        y = y + dexp_ref[:, g * gw:(g + 1) * gw] * xs

        pieces = []
        for j in range(hpg // 2):
            xpair = xdt[:, j * 2 * SSD_HEAD_DIM:(j + 1) * 2 * SSD_HEAD_DIM]
            acc = None
            for k in range(2):
                hd = g * hpg + 2 * j + k
                diff = a_cum[:, hd:hd + 1] - a_cum_t[hd:hd + 1, :]
                decay = jnp.where(causal, jnp.exp(jnp.where(causal, diff, 0.0)), 0.0)
                mat = (cb * decay).astype(BF16)
                keep = first_head if k == 0 else jnp.logical_not(first_head)
                term = _dot(mat, jnp.where(keep, xpair, jnp.zeros_like(xpair)))
                acc = term if acc is None else acc + term
            pieces.append(acc)
        y = y + jnp.concatenate(pieces, axis=1)

        zg = z_ref[0, :, g * gw:(g + 1) * gw].astype(F32)
        y = y * (zg * jax.nn.sigmoid(zg))
        y = _rms(y, ng_ref[:, g * gw:(g + 1) * gw])
        o_ref[0, :, g * gw:(g + 1) * gw] = y.astype(BF16)


def _ssd(z, xbc, dt, dtt, cw, cb, bias, biast, alog, alogt, dexp, ng, tril, triu, emat):
    b, l, d_inner = z.shape
    n_heads = dt.shape[2]
    conv_dim = xbc.shape[2]
    q = SSD_CHUNK
    gw = d_inner // SSD_N_GROUPS
    kern = functools.partial(_ssd_kernel, d_inner=d_inner, n_heads=n_heads)
    return pl.pallas_call(
        kern,
        grid=(b, l // q),
        in_specs=[
            pl.BlockSpec((1, q, d_inner), lambda i, j: (i, j, 0)),
            pl.BlockSpec((1, q, conv_dim), lambda i, j: (i, j, 0)),
            pl.BlockSpec((1, q, n_heads), lambda i, j: (i, j, 0)),
            pl.BlockSpec((1, n_heads, q), lambda i, j: (i, 0, j)),
            _resident(cw.shape), _resident(cb.shape), _resident(bias.shape), _resident(biast.shape),
            _resident(alog.shape), _resident(alogt.shape), _resident(dexp.shape), _resident(ng.shape),
            _resident(tril.shape), _resident(triu.shape), _resident(emat.shape),
        ],
        out_specs=pl.BlockSpec((1, q, d_inner), lambda i, j: (i, j, 0)),
        out_shape=jax.ShapeDtypeStruct((b, l, d_inner), BF16),
        scratch_shapes=[pltpu.VMEM((q + SUBLANES, conv_dim), F32),
                        pltpu.VMEM((SSD_N_GROUPS, SSD_D_STATE, gw), F32)],
        compiler_params=_params(("parallel", "arbitrary")),
        name="ssd",
    )(z, xbc, dt, dtt, cw, cb, bias, biast, alog, alogt, dexp, ng, tril, triu, emat)


def _s5_kernel(u_ref, perm_ref, permt_ref, bmat_ref, cmat_ref, are_ref, aim_ref, d_ref,
               o_ref, ut, bu, st, *, tb):
    nb = u_ref.shape[0]
    ns = S5_SLAB_STATE
    pr = nb * S5_PERM_T

    @pl.when(pl.program_id(1) == 0)
    def _():
        st[...] = jnp.zeros(st.shape, F32)

    for k in range(tb // S5_PERM_T):
        blk = u_ref[:, k * S5_PERM_T:(k + 1) * S5_PERM_T, :].reshape(pr, LANES)
        ut[k * pr:(k + 1) * pr, :] = _dot(perm_ref[...], blk).astype(BF16)

    bu[...] = _dot(ut[...], bmat_ref[0])

    a_re = jnp.broadcast_to(are_ref[0], (nb, ns))
    a_im = jnp.broadcast_to(aim_ref[0], (nb, ns))

    def step(t, carry):
        s_re, s_im = carry
        r = pl.multiple_of(t * nb, nb)
        n_re = a_re * s_re - a_im * s_im + bu[pl.ds(r, nb), 0:ns]
        n_im = a_re * s_im + a_im * s_re + bu[pl.ds(r, nb), ns:2 * ns]
        bu[pl.ds(r, nb), 0:ns] = n_re
        bu[pl.ds(r, nb), ns:2 * ns] = n_im
        return n_re, n_im

    s_re, s_im = lax.fori_loop(0, tb, step, (st[:, 0:ns], st[:, ns:2 * ns]), unroll=8)
    st[:, 0:ns] = s_re
    st[:, ns:2 * ns] = s_im

    y = _dot(bu[...].astype(BF16), cmat_ref[0])
    y = y + d_ref[0] * ut[...].astype(F32)
    y = jax.nn.gelu(y, approximate=True).astype(BF16)
    for k in range(tb // S5_PERM_T):
        blk = _dot(permt_ref[...], y[k * pr:(k + 1) * pr, :]).astype(BF16)
        o_ref[:, k * S5_PERM_T:(k + 1) * S5_PERM_T, :] = blk.reshape(nb, S5_PERM_T, LANES)


def _s5(u, perm, permt, bmat, cmat, a_re, a_im, dskip, tb):
    b, l, w = u.shape
    n_slab = w // LANES
    ns2 = 2 * S5_SLAB_STATE
    kern = functools.partial(_s5_kernel, tb=tb)
    blk = pl.BlockSpec((b, tb, LANES), lambda s, t: (0, t, s))
    return pl.pallas_call(
        kern,
        grid=(n_slab, l // tb),
        in_specs=[
            blk, _resident(perm.shape), _resident(permt.shape),
            pl.BlockSpec((1, LANES, ns2), lambda s, t: (s, 0, 0)),
            pl.BlockSpec((1, ns2, LANES), lambda s, t: (s, 0, 0)),
            pl.BlockSpec((1, 1, S5_SLAB_STATE), lambda s, t: (s, 0, 0)),
            pl.BlockSpec((1, 1, S5_SLAB_STATE), lambda s, t: (s, 0, 0)),
            pl.BlockSpec((1, 1, LANES), lambda s, t: (s, 0, 0)),
        ],
        out_specs=blk,
        out_shape=jax.ShapeDtypeStruct((b, l, w), BF16),
        scratch_shapes=[pltpu.VMEM((tb * b, LANES), BF16),
                        pltpu.VMEM((tb * b, ns2), F32),
                        pltpu.VMEM((b, ns2), F32)],
        compiler_params=_params(("parallel", "arbitrary")),
        name="s5",
    )(u, perm, permt, bmat, cmat, a_re, a_im, dskip)


def _s5_operators(lam_re, lam_im, b_re, b_im, c_re, c_im, log_step):
    n_groups = lam_re.shape[0]
    n_slab = n_groups // S5_SLAB_GROUPS
    lre = jnp.minimum(lam_re, S5_MAX_REAL)
    step = jnp.exp(log_step)[:, None]
    xr = lre * step
    ang = lam_im * step
    mag = jnp.exp(xr)
    lb_re = mag * jnp.cos(ang)
    lb_im = mag * jnp.sin(ang)
    nr = jnp.expm1(xr) * jnp.cos(ang) - 2.0 * jnp.sin(0.5 * ang) ** 2
    ni = lb_im
    den = lre * lre + lam_im * lam_im
    q_re = (nr * lre + ni * lam_im) / den
    q_im = (ni * lre - nr * lam_im) / den
    bb_re = q_re[..., None] * b_re - q_im[..., None] * b_im
    bb_im = q_re[..., None] * b_im + q_im[..., None] * b_re
    eye = jnp.eye(S5_SLAB_GROUPS, dtype=F32)

    def in_map(bb):
        bb = bb.reshape(n_slab, S5_SLAB_GROUPS, S5_STATE, S5_GROUP)
        return jnp.einsum("sgnc,gh->sgchn", bb, eye).reshape(n_slab, LANES, S5_SLAB_STATE)

    def out_map(cc):
        cc = cc.reshape(n_slab, S5_SLAB_GROUPS, S5_GROUP, S5_STATE)
        return jnp.einsum("sgcn,gh->sgnhc", cc, eye).reshape(n_slab, S5_SLAB_STATE, LANES)

    bmat = jnp.concatenate([in_map(bb_re), in_map(bb_im)], axis=2).astype(BF16)
    cmat = jnp.concatenate([out_map(c_re), -out_map(c_im)], axis=1).astype(BF16)
    a_re = lb_re.reshape(n_slab, 1, S5_SLAB_STATE)
    a_im = lb_im.reshape(n_slab, 1, S5_SLAB_STATE)
    return bmat, cmat, a_re, a_im


def _merge_kernel(h_ref, ya_ref, g5_ref, ga_ref, gb_ref, post_ref, wa_ref, wglu_ref, wb_ref, wo_ref,
                  o_ref):
    d = h_ref.shape[1]
    ya = _dot(ya_ref[...], wa_ref[...])
    glu = _dot(g5_ref[...], wglu_ref[...])
    width = glu.shape[1] // 2
    yb_in = (glu[:, :width] * jax.nn.sigmoid(glu[:, width:])).astype(BF16)
    yb = _dot(yb_in, wb_ref[...])
    merged = (jax.nn.sigmoid(ga_ref[...].astype(F32)) * ya
              + jax.nn.sigmoid(gb_ref[...].astype(F32)) * yb)
    m = _dot(merged.astype(BF16), wo_ref[...])
    del d
    o_ref[...] = h_ref[...] + _rms(m, post_ref[...])


def _merge(h, ya, g5, ga, gb, post_g, wa, wglu, wb, wo, tm):
    m, d = h.shape
    row = lambda n: pl.BlockSpec((tm, n), lambda i: (i, 0))
    return pl.pallas_call(
        _merge_kernel,
        grid=(m // tm,),
        in_specs=[row(d), row(ya.shape[1]), row(g5.shape[1]), row(d), row(d), _resident((1, d)),
                  _resident(wa.shape), _resident(wglu.shape), _resident(wb.shape), _resident(wo.shape)],
        out_specs=row(d),
        out_shape=jax.ShapeDtypeStruct((m, d), F32),
        compiler_params=_params(("parallel",)),
        name="merge",
    )(h, ya, g5, ga, gb, post_g, wa, wglu, wb, wo)


def _tri_constants():
    q = SSD_CHUNK
    r = jnp.arange(q)
    tril = (r[:, None] >= r[None, :]).astype(BF16)
    return tril, tril.T


def _perm_constants(nb):
    n = nb * S5_PERM_T
    r = jnp.arange(n)
    src = (r % nb) * S5_PERM_T + r // nb
    perm = (src[:, None] == jnp.arange(n)[None, :]).astype(BF16)
    return perm, perm.T


def kernel(x, ffn1_pre_g, ffn1_post_g, ffn1_w_gate, ffn1_w_up, ffn1_w_down, mix_pre_g, mix_post_g, w_in, ssd_conv_w, ssd_conv_b, ssd_dt_bias, ssd_a_log, ssd_d, ssd_norm_g, w_branch_a, s5_lambda_re, s5_lambda_im, s5_b_re, s5_b_im, s5_c_re, s5_c_im, s5_log_step, s5_d, s5_w_glu, w_branch_b, w_out, ffn2_pre_g, ffn2_post_g, ffn2_w_gate, ffn2_w_up, ffn2_w_down):
    b, l, d = x.shape
    depth = w_in.shape[0]
    n_heads = ssd_dt_bias.shape[1]
    d_inner = ssd_norm_g.shape[1]
    conv_dim = ssd_conv_b.shape[1]
    s5_width = s5_d.shape[1]
    m = b * l
    tm = min(512, m)
    tb = min(256, l)

    splits = [0, d_inner, d_inner + conv_dim, d_inner + conv_dim + n_heads,
              d_inner + conv_dim + n_heads + s5_width, d_inner + conv_dim + n_heads + s5_width + d,
              w_in.shape[2]]
    tril, triu = _tri_constants()
    perm, permt = _perm_constants(b)
    emat = (jnp.arange(n_heads)[:, None] == (jnp.arange(d_inner) // SSD_HEAD_DIM)[None, :]).astype(BF16)

    bf = lambda w: w.astype(BF16)
    vec = lambda v: v.reshape(1, -1)

    h = x.reshape(m, d)
    for i in range(depth):
        h = _ffn(h, vec(ffn1_pre_g[i]), vec(ffn1_post_g[i]),
                 bf(ffn1_w_gate[i]), bf(ffn1_w_up[i]), bf(ffn1_w_down[i]), tm)

        ws = [bf(w_in[i][:, splits[k]:splits[k + 1]]) for k in range(6)]
        z, xbc, dt, u5, ga, gb = _inproj(h, vec(mix_pre_g[i]), ws, tm)

        dt3 = dt.reshape(b, l, n_heads)
        ya = _ssd(z.reshape(b, l, d_inner), xbc.reshape(b, l, conv_dim), dt3, jnp.swapaxes(dt3, 1, 2),
                  ssd_conv_w[i], vec(ssd_conv_b[i]), vec(ssd_dt_bias[i]), ssd_dt_bias[i].reshape(-1, 1),
                  vec(ssd_a_log[i]), ssd_a_log[i].reshape(-1, 1),
                  vec(jnp.repeat(ssd_d[i], SSD_HEAD_DIM)), vec(ssd_norm_g[i]), tril, triu, emat)

        bmat, cmat, a_re, a_im = _s5_operators(s5_lambda_re[i], s5_lambda_im[i], s5_b_re[i], s5_b_im[i],
                                               s5_c_re[i], s5_c_im[i], s5_log_step[i])
        g5 = _s5(u5.reshape(b, l, s5_width), perm, permt, bmat, cmat, a_re, a_im,
                 s5_d[i].reshape(-1, 1, LANES), tb)

        h = _merge(h, ya.reshape(m, d_inner), g5.reshape(m, s5_width), ga, gb, vec(mix_post_g[i]),
                   bf(w_branch_a[i]), bf(s5_w_glu[i]), bf(w_branch_b[i]), bf(w_out[i]), tm)

        h = _ffn(h, vec(ffn2_pre_g[i]), vec(ffn2_post_g[i]),
                 bf(ffn2_w_gate[i]), bf(ffn2_w_up[i]), bf(ffn2_w_down[i]), tm)
    return h.reshape(b, l, d)
```
